```python
import math
import jax
import jax.numpy as jnp
from jax import lax
import numpy as np

D_MODEL = 1024
BATCH = 8
SEQ = 4096
DEPTH = 2

GRID_W = 64
CTX_LEN = 256
EPS = 1e-6
N_MOD = 6

NA_HEADS = 8
NA_HEAD_DIM = 64
NA_WIDTH = NA_HEADS * NA_HEAD_DIM
NA_KH_MAX = 8
NA_KW = 16

HY_WIDTH = 256
HY_SHORT = 3
HY_BANDS = 16
HY_EMB = 1 + 2 * HY_BANDS
HY_HIDDEN = 64
HY_FAST_DECAY = 0.3
HY_SLOW_DECAY = 1.5
HY_TARGET = 1e-2

RG_WIDTH = 256
RG_BLOCKS = 4
RG_BLOCK_DIM = RG_WIDTH // RG_BLOCKS
RG_CONV = 4
RG_C = 8.0

N_BRANCH = 3
HY_OFF = 3 * NA_WIDTH
RG_OFF = HY_OFF + 3 * HY_WIDTH
IN_COLS = RG_OFF + 2 * RG_WIDTH

PEER_HEADS = 8
PEER_DK = 256
PEER_N_KEYS = 128
PEER_TOPK = 16
PEER_N_EXPERTS = PEER_N_KEYS * PEER_N_KEYS
PEER_CHUNK = 128

kernel_name = 'hybrid_na_hyena_rglru_peer_dit'


def rmsnorm(x, g):
    xf = x.astype(jnp.float32)
    y = xf * lax.rsqrt(jnp.mean(xf * xf, axis=-1, keepdims=True) + EPS)
    return (y * g.astype(jnp.float32)).astype(x.dtype)


def dwconv(x, w, b, pad):
    y = lax.conv_general_dilated(x, w[:, None, :].astype(x.dtype), window_strides=(1,), padding=[pad],
                                 dimension_numbers=('NWC', 'WIO', 'NWC'), feature_group_count=x.shape[-1])
    return y + b.astype(x.dtype)


def neighbourhood_attention(q, k, v, kc, vc, rpb):
    B, T, H, hd = q.shape
    rows = T // GRID_W
    kh = min(NA_KH_MAX, rows)
    scale = hd ** -0.5
    qg = q.reshape(B, rows, GRID_W, H, hd)
    kg = k.reshape(B, rows, GRID_W, H, hd)
    vg = v.reshape(B, rows, GRID_W, H, hd)
    row_start = jnp.clip(jnp.arange(rows) - kh // 2, 0, rows - kh)
    cols = jnp.arange(GRID_W)
    col_keys = jnp.clip(cols - NA_KW // 2, 0, GRID_W - NA_KW)[:, None] + jnp.arange(NA_KW)[None, :]
    dc_idx = col_keys - cols[:, None] + (NA_KW - 1)

    def row_block(r):
        rs = row_start[r]
        q_r = lax.dynamic_index_in_dim(qg, r, axis=1, keepdims=False)
        k_win = lax.dynamic_slice_in_dim(kg, rs, kh, axis=1)[:, :, col_keys]
        v_win = lax.dynamic_slice_in_dim(vg, rs, kh, axis=1)[:, :, col_keys]
        dr_idx = rs + jnp.arange(kh) - r + (NA_KH_MAX - 1)
        bias = rpb[:, dr_idx[:, None, None], dc_idx[None]].transpose(0, 2, 1, 3)
        s_loc = jnp.einsum('bwhd,bawkhd->bhwak', q_r, k_win) * scale + bias
        s_loc = s_loc.reshape(B, H, GRID_W, kh * NA_KW)
        s_ctx = jnp.einsum('bwhd,bchd->bhwc', q_r, kc) * scale
        s = jnp.concatenate([s_loc.astype(jnp.float32), s_ctx.astype(jnp.float32)], axis=-1)
        p = jax.nn.softmax(s, axis=-1).astype(v.dtype)
        p_loc = p[..., :kh * NA_KW].reshape(B, H, GRID_W, kh, NA_KW)
        p_ctx = p[..., kh * NA_KW:]
        return (jnp.einsum('bhwak,bawkhd->bwhd', p_loc, v_win)
                + jnp.einsum('bhwc,bchd->bwhd', p_ctx, vc))

    out = lax.map(row_block, jnp.arange(rows))
    return out.transpose(1, 0, 2, 3, 4).reshape(B, T, H * hd)


def context_attention(q, k, v):
    scale = q.shape[-1] ** -0.5
    s = jnp.einsum('bqhd,bkhd->bhqk', q, k).astype(jnp.float32) * scale
    p = jax.nn.softmax(s, axis=-1).astype(v.dtype)
    o = jnp.einsum('bhqk,bkhd->bqhd', p, v)
    return o.reshape(q.shape[0], q.shape[1], -1)


def hyena_filters(L, f1_w, f1_b, f2_w, f2_b, f3_w, freq):
    f32 = jnp.float32
    t = jnp.linspace(0.0, 1.0, L, dtype=f32)[:, None]
    w = (2.0 * math.pi / L) * jnp.arange(L, dtype=f32)[:, None]
    bands = jnp.linspace(1e-4, HY_BANDS - 1, HY_BANDS, dtype=f32)[None, :]
    z = jnp.concatenate([t, jnp.cos(bands * w), -jnp.sin(bands * w)], axis=-1)
    h = jnp.sin(freq[0].astype(f32) * (z @ f1_w.astype(f32) + f1_b.astype(f32)))
    h = jnp.sin(freq[1].astype(f32) * (h @ f2_w.astype(f32) + f2_b.astype(f32)))
    h = h @ f3_w.astype(f32)
    deltas = jnp.linspace(math.log(HY_TARGET) / HY_SLOW_DECAY, math.log(HY_TARGET) / HY_FAST_DECAY,
                          HY_WIDTH, dtype=f32)
    decay = jnp.exp(-t * jnp.abs(deltas)[None, :])
    h = h.reshape(L, 2, HY_WIDTH) * decay[:, None, :]
    return h[:, 0], h[:, 1]


def bidirectional_long_conv(u, h_fwd, h_bwd):
    L = u.shape[1]
    k = jnp.concatenate([h_fwd, jnp.zeros_like(h_fwd[:1]), jnp.flip(h_bwd[1:], axis=0)], axis=0)
    U = jnp.fft.rfft(u.astype(jnp.float32), n=2 * L, axis=1)
    K = jnp.fft.rfft(k, n=2 * L, axis=0)
    return jnp.fft.irfft(U * K[None], n=2 * L, axis=1)[:, :L]


def hyena(p, short_w, short_b, f1_w, f1_b, f2_w, f2_b, f3_w, freq, d_skip):
    L = p.shape[1]
    pc = dwconv(p, short_w, short_b, (HY_SHORT // 2, HY_SHORT // 2))
    x0, x1, v = jnp.split(pc, 3, axis=-1)
    u = (v * x1).astype(jnp.float32)
    h_fwd, h_bwd = hyena_filters(L, f1_w, f1_b, f2_w, f2_b, f3_w, freq)
    y = bidirectional_long_conv(u, h_fwd, h_bwd) + u * d_skip.astype(jnp.float32)
    return (x0.astype(jnp.float32) * y).astype(p.dtype)


def _lru_combine(e1, e2):
    a1, b1 = e1
    a2, b2 = e2
    return a1 * a2, a2 * b1 + b2


def rglru_coeffs(xc, wa, ba, wx, bx, lam):
    B, L, _ = xc.shape
    f32 = jnp.float32
    xb = xc.reshape(B, L, RG_BLOCKS, RG_BLOCK_DIM)
    r = jax.nn.sigmoid(jnp.einsum('blnd,nde->blne', xb, wa.astype(f32)).reshape(B, L, RG_WIDTH) + ba.astype(f32))
    i = jax.nn.sigmoid(jnp.einsum('blnd,nde->blne', xb, wx.astype(f32)).reshape(B, L, RG_WIDTH) + bx.astype(f32))
    log_a = -RG_C * r * jax.nn.softplus(-lam.astype(f32))
    a = jnp.exp(log_a)
    return a, jnp.sqrt(-jnp.expm1(2.0 * log_a)) * (i * xc)


def linear_scan(a, b, h0):
    b = b.at[:, 0].add(a[:, 0] * h0)
    return lax.associative_scan(_lru_combine, (a, b), axis=1)[1]


def rglru_bidirectional(x, conv_w, conv_b, wa, ba, wx, bx, lam, h0_fwd, h0_bwd):
    xc = dwconv(x, conv_w, conv_b, (RG_CONV // 2, RG_CONV - 1 - RG_CONV // 2)).astype(jnp.float32)
    a_f, b_f = rglru_coeffs(xc, wa[0], ba[0], wx[0], bx[0], lam[0])
    a_b, b_b = rglru_coeffs(xc, wa[1], ba[1], wx[1], bx[1], lam[1])
    h_fwd = linear_scan(a_f, b_f, h0_fwd)
    h_bwd = jnp.flip(linear_scan(jnp.flip(a_b, 1), jnp.flip(b_b, 1), h0_bwd), 1)
    return h_fwd, h_bwd


def gated_merge(xn, ya, yb, yc, w_gate, b_gate, w_br_a, w_br_b, w_br_c, w_out):
    g = jax.nn.sigmoid((xn @ w_gate + b_gate).astype(jnp.float32)).astype(xn.dtype)
    g = g.reshape(xn.shape[:-1] + (N_BRANCH, D_MODEL))
    m = g[..., 0, :] * (ya @ w_br_a) + g[..., 1, :] * (yb @ w_br_b) + g[..., 2, :] * (yc @ w_br_c)
    return m @ w_out


def peer(xt, wq, keys, u, v):
    N, D = xt.shape

    def chunk(xb):
        q = (xb @ wq).reshape(PEER_CHUNK, PEER_HEADS, 2, PEER_DK // 2)
        s = jnp.einsum('thpd,hpkd->thpk', q, keys).astype(jnp.float32)
        s_top, i_top = lax.top_k(s, PEER_TOPK)
        cand = (s_top[:, :, 0, :, None] + s_top[:, :, 1, None, :]).reshape(PEER_CHUNK, PEER_HEADS, PEER_TOPK * PEER_TOPK)
        cand_idx = (i_top[:, :, 0, :, None] * PEER_N_KEYS + i_top[:, :, 1, None, :]).reshape(PEER_CHUNK, PEER_HEADS, PEER_TOPK * PEER_TOPK)
        best, pos = lax.top_k(cand, PEER_TOPK)
        idx = jnp.take_along_axis(cand_idx, pos, axis=-1)
        g = jax.nn.softmax(best, axis=-1)
        u_sel = jnp.take(u, idx, axis=0)
        v_sel = jnp.take(v, idx, axis=0)
        act = jax.nn.gelu(jnp.einsum('thkd,td->thk', u_sel, xb).astype(jnp.float32)) * g
        return jnp.einsum('thk,thkd->td', act.astype(xb.dtype), v_sel)

    return lax.map(chunk, xt.reshape(N // PEER_CHUNK, PEER_CHUNK, D)).reshape(N, D)


def setup_inputs(seed: int = 0) -> dict:
    key = jax.random.key(seed)
    ks = iter(jax.random.split(key, 48))
    f32 = jnp.float32
    D = D_MODEL

    def nrm(shape, scale):
        return jax.random.normal(next(ks), shape, f32) * scale

    a8 = jax.random.uniform(next(ks), (DEPTH, 2, RG_WIDTH), f32, 0.9, 0.999)
    a_base = a8 ** (1.0 / RG_C)
    return {
        'x': nrm((BATCH, SEQ, D), 1.0),
        'c': nrm((BATCH, D), 1.0),
        'ctx': nrm((BATCH, CTX_LEN, D), 1.0),
        'c_ctx': nrm((D,), 1.0),
        'norm_mix_g': 1.0 + nrm((DEPTH, D), 0.02),
        'norm_ffn_g': 1.0 + nrm((DEPTH, D), 0.02),
        'w_ada': nrm((DEPTH, D, N_MOD * D), 0.5 * D ** -0.5),
        'b_ada': nrm((DEPTH, N_MOD * D), 0.02),
        'w_in': nrm((DEPTH, D, IN_COLS), D ** -0.5),
        'na_rpb': nrm((DEPTH, NA_HEADS, 2 * NA_KH_MAX - 1, 2 * NA_KW - 1), 0.5),
        'hy_short_w': nrm((DEPTH, HY_SHORT, 3 * HY_WIDTH), HY_SHORT ** -0.5),
        'hy_short_b': nrm((DEPTH, 3 * HY_WIDTH), 0.02),
        'hy_f1_w': nrm((DEPTH, HY_EMB, HY_HIDDEN), HY_EMB ** -0.5),
        'hy_f1_b': nrm((DEPTH, HY_HIDDEN), 0.1),
        'hy_f2_w': nrm((DEPTH, HY_HIDDEN, HY_HIDDEN), HY_HIDDEN ** -0.5),
        'hy_f2_b': nrm((DEPTH, HY_HIDDEN), 0.1),
        'hy_f3_w': nrm((DEPTH, HY_HIDDEN, 2 * HY_WIDTH), 0.01),
        'hy_freq': 1.0 + nrm((DEPTH, 2, HY_HIDDEN), 0.1),
        'hy_bias': nrm((DEPTH, HY_WIDTH), 0.1),
        'rg_conv_w': nrm((DEPTH, RG_CONV, RG_WIDTH), RG_CONV ** -0.5),
        'rg_conv_b': nrm((DEPTH, RG_WIDTH), 0.02),
        'rg_wa': nrm((DEPTH, 2, RG_BLOCKS, RG_BLOCK_DIM, RG_BLOCK_DIM), RG_BLOCK_DIM ** -0.5),
        'rg_ba': nrm((DEPTH, 2, RG_WIDTH), 0.1),
        'rg_wx': nrm((DEPTH, 2, RG_BLOCKS, RG_BLOCK_DIM, RG_BLOCK_DIM), RG_BLOCK_DIM ** -0.5),
        'rg_bx': nrm((DEPTH, 2, RG_WIDTH), 0.1),
        'rg_lambda': jnp.log(a_base) - jnp.log1p(-a_base),
        'w_gate': nrm((DEPTH, D, N_BRANCH * D), D ** -0.5),
        'b_gate': nrm((DEPTH, N_BRANCH * D), 0.1),
        'w_br_a': nrm((DEPTH, NA_WIDTH, D), NA_WIDTH ** -0.5),
        'w_br_b': nrm((DEPTH, HY_WIDTH, D), HY_WIDTH ** -0.5),
        'w_br_c': nrm((DEPTH, RG_WIDTH, D), RG_WIDTH ** -0.5),
        'w_out': nrm((DEPTH, D, D), D ** -0.5),
        'peer_wq': nrm((DEPTH, D, PEER_HEADS * PEER_DK), D ** -0.5),
        'peer_keys': nrm((DEPTH, PEER_HEADS, 2, PEER_N_KEYS, PEER_DK // 2), (PEER_DK // 2) ** -0.5),
        'peer_u': nrm((DEPTH, PEER_N_EXPERTS, D), D ** -0.5),
        'peer_v': nrm((DEPTH, PEER_N_EXPERTS, D), 0.5),
        'final_g': 1.0 + nrm((D,), 0.02),
    }


def reference(x, c, ctx, c_ctx, norm_mix_g, norm_ffn_g, w_ada, b_ada, w_in, na_rpb,
              hy_short_w, hy_short_b, hy_f1_w, hy_f1_b, hy_f2_w, hy_f2_b, hy_f3_w, hy_freq, hy_bias,
              rg_conv_w, rg_conv_b, rg_wa, rg_ba, rg_wx, rg_bx, rg_lambda,
              w_gate, b_gate, w_br_a, w_br_b, w_br_c, w_out,
              peer_wq, peer_keys, peer_u, peer_v, final_g):
    B, T, D = x.shape
    C = ctx.shape[1]
    H, hd = NA_HEADS, NA_HEAD_DIM
    for l in range(DEPTH):
        last = l == DEPTH - 1
        mod = (jax.nn.silu(c) @ w_ada[l] + b_ada[l]).reshape(B, 1, N_MOD, D)
        mod_c = (jax.nn.silu(c_ctx) @ w_ada[l] + b_ada[l]).reshape(1, 1, N_MOD, D)
        hy_args = (hy_short_w[l], hy_short_b[l], hy_f1_w[l], hy_f1_b[l], hy_f2_w[l], hy_f2_b[l],
                   hy_f3_w[l], hy_freq[l], hy_bias[l])
        rg_args = (rg_conv_w[l], rg_conv_b[l], rg_wa[l], rg_ba[l], rg_wx[l], rg_bx[l], rg_lambda[l])
        merge_args = (w_gate[l], b_gate[l], w_br_a[l], w_br_b[l], w_br_c[l], w_out[l])

        xn = rmsnorm(x, norm_mix_g[l]) * (1.0 + mod[:, :, 1]) + mod[:, :, 0]
        cn = rmsnorm(ctx, norm_mix_g[l]) * (1.0 + mod_c[:, :, 1]) + mod_c[:, :, 0]
        p = xn @ w_in[l]
        pc = cn @ w_in[l]
        q = p[..., 0:NA_WIDTH].reshape(B, T, H, hd)
        k = p[..., NA_WIDTH:2 * NA_WIDTH].reshape(B, T, H, hd)
        v = p[..., 2 * NA_WIDTH:3 * NA_WIDTH].reshape(B, T, H, hd)
        qc = pc[..., 0:NA_WIDTH].reshape(B, C, H, hd)
        kc = pc[..., NA_WIDTH:2 * NA_WIDTH].reshape(B, C, H, hd)
        vc = pc[..., 2 * NA_WIDTH:3 * NA_WIDTH].reshape(B, C, H, hd)
        ya = neighbourhood_attention(q, k, v, kc, vc, na_rpb[l])
        yb = hyena(p[..., HY_OFF:RG_OFF], *hy_args)
        h0 = jnp.zeros((B, RG_WIDTH), jnp.float32)
        hf_c, hb_c = rglru_bidirectional(pc[..., RG_OFF:RG_OFF + RG_WIDTH], *rg_args, h0, h0)
        hf, hb = rglru_bidirectional(p[..., RG_OFF:RG_OFF + RG_WIDTH], *rg_args, hf_c[:, -1], hb_c[:, 0])
        yc = ((hf + hb) * jax.nn.gelu(p[..., RG_OFF + RG_WIDTH:IN_COLS].astype(jnp.float32))).astype(x.dtype)
        x = x + (mod[:, :, 2] * gated_merge(xn, ya, yb, yc, *merge_args)).astype(x.dtype)
        if not last:
            ya_c = context_attention(qc, kc, vc)
            yb_c = hyena(pc[..., HY_OFF:RG_OFF], *hy_args)
            yc_c = ((hf_c + hb_c) * jax.nn.gelu(pc[..., RG_OFF + RG_WIDTH:IN_COLS].astype(jnp.float32))).astype(ctx.dtype)
            ctx = ctx + (mod_c[:, :, 2] * gated_merge(cn, ya_c, yb_c, yc_c, *merge_args)).astype(ctx.dtype)

        xn = rmsnorm(x, norm_ffn_g[l]) * (1.0 + mod[:, :, 4]) + mod[:, :, 3]
        y = peer(xn.reshape(B * T, D), peer_wq[l], peer_keys[l], peer_u[l], peer_v[l]).reshape(B, T, D)
        x = x + (mod[:, :, 5] * y).astype(x.dtype)
        if not last:
            cn = rmsnorm(ctx, norm_ffn_g[l]) * (1.0 + mod_c[:, :, 4]) + mod_c[:, :, 3]
            yc_f = peer(cn.reshape(B * C, D), peer_wq[l], peer_keys[l], peer_u[l], peer_v[l]).reshape(B, C, D)
            ctx = ctx + (mod_c[:, :, 5] * yc_f).astype(ctx.dtype)
    return rmsnorm(x, final_g)
```

```python
import functools
import math

import numpy as np
import jax
import jax.numpy as jnp
from jax import lax
from jax.experimental import pallas as pl
from jax.experimental.pallas import tpu as pltpu

F32 = jnp.float32
BF16 = jnp.bfloat16
HIGHEST = lax.Precision.HIGHEST

GRID_W = 64
EPS = 1e-6
N_MOD = 6
NA_HEADS = 8
NA_HEAD_DIM = 64
NA_KH_MAX = 8
NA_KW = 16
HY_WIDTH = 256
HY_SHORT = 3
HY_BANDS = 16
HY_HIDDEN = 64
HY_FAST_DECAY = 0.3
HY_SLOW_DECAY = 1.5
HY_TARGET = 1e-2
RG_WIDTH = 256
RG_BLOCKS = 4
RG_CONV = 4
RG_C = 8.0
PEER_HEADS = 8
PEER_DK = 256
PEER_N_KEYS = 128
PEER_TOPK = 16

V7X_LANES = 128
V7X_SUBLANES = 8
V7X_VMEM_LIMIT_BYTES = 56 * 1024 * 1024

MASKED_SCORE = -1e30


def _cparams(*sem):
    return pltpu.CompilerParams(dimension_semantics=sem, vmem_limit_bytes=V7X_VMEM_LIMIT_BYTES)


def _dot(a, b):
    return jnp.dot(a, b, preferred_element_type=F32)


def _dot_f32(a, b):
    return jnp.dot(a, b, precision=HIGHEST, preferred_element_type=F32)


def _dot_nt(a, b):
    return lax.dot_general(a, b, (((1,), (1,)), ((), ())), preferred_element_type=F32)


def _gelu_tanh(x):
    return 0.5 * x * (1.0 + jnp.tanh(math.sqrt(2.0 / math.pi) * (x + 0.044715 * (x * x * x))))


def _mod_norm(x, g, scale, shift):
    ms = jnp.mean(x * x, axis=-1, keepdims=True)
    return (x * lax.rsqrt(ms + EPS) * g) * (1.0 + scale) + shift


def _ada_kernel(c_ref, w_ref, b_ref, o_ref):
    c = c_ref[...]
    o_ref[0] = _dot_f32(c * jax.nn.sigmoid(c), w_ref[0]) + b_ref[0]


def _ada_mods(c, c_ctx, w_ada, b_ada):
    depth, d, nd = w_ada.shape
    b = c.shape[0]
    rows = -(-(b + 1) // V7X_SUBLANES) * V7X_SUBLANES
    cc = jnp.zeros((rows, d), F32).at[:b].set(c).at[b].set(c_ctx)
    tn = math.gcd(nd, 1536)
    out = pl.pallas_call(
        _ada_kernel,
        out_shape=jax.ShapeDtypeStruct((depth, rows, nd), F32),
        grid=(depth, nd // tn),
        in_specs=[pl.BlockSpec((rows, d), lambda l, j: (0, 0)),
                  pl.BlockSpec((1, d, tn), lambda l, j: (l, 0, j)),
                  pl.BlockSpec((1, 1, tn), lambda l, j: (l, 0, j))],
        out_specs=pl.BlockSpec((1, rows, tn), lambda l, j: (l, 0, j)),
        compiler_params=_cparams("parallel", "parallel"),
        name="ada_mods",
    )(cc, w_ada, b_ada.reshape(depth, 1, nd))
    mod = out[:, :b].reshape(depth, b, N_MOD, d)
    mod_c = jnp.broadcast_to(out[:, b].reshape(depth, 1, N_MOD, d), (depth, b, N_MOD, d))
    return mod, mod_c


def _inproj_kernel(x_ref, g_ref, mod_ref, w_ref, qkv_ref, rest_ref, *, na_width, qscale):
    mod = mod_ref[0]
    xn = _mod_norm(x_ref[0], g_ref[...], mod[1:2], mod[0:1])
    p = _dot(xn.astype(BF16), w_ref[...])
    qkv_ref[0, :, :na_width] = (p[:, :na_width] * qscale).astype(BF16)
    qkv_ref[0, :, na_width:] = p[:, na_width:3 * na_width].astype(BF16)
    rest_ref[0] = p[:, 3 * na_width:]


def _inproj(x, g, mod, w_in_bf):
    b, t, d = x.shape
    ncol = w_in_bf.shape[1]
    na_width = NA_HEADS * NA_HEAD_DIM
    nrest = ncol - 3 * na_width
    tm = min(t, 512)
    kern = functools.partial(_inproj_kernel, na_width=na_width, qscale=NA_HEAD_DIM ** -0.5)
    return pl.pallas_call(
        kern,
        out_shape=(jax.ShapeDtypeStruct((b, t, 3 * na_width), BF16),
                   jax.ShapeDtypeStruct((b, t, nrest), F32)),
        grid=(b, t // tm),
        in_specs=[pl.BlockSpec((1, tm, d), lambda i, j: (i, j, 0)),
                  pl.BlockSpec((1, d), lambda i, j: (0, 0)),
                  pl.BlockSpec((1, N_MOD, d), lambda i, j: (i, 0, 0)),
                  pl.BlockSpec((d, ncol), lambda i, j: (0, 0))],
        out_specs=(pl.BlockSpec((1, tm, 3 * na_width), lambda i, j: (i, j, 0)),
                   pl.BlockSpec((1, tm, nrest), lambda i, j: (i, j, 0))),
        compiler_params=_cparams("parallel", "parallel"),
        name="inproj",
    )(x, g.reshape(1, d), mod, w_in_bf)


def _na_bias_table(rpb, rows):
    kh = min(NA_KH_MAX, rows)
    w = GRID_W
    cols = np.arange(w)
    c0 = np.clip(cols - NA_KW // 2, 0, w - NA_KW)
    kc = np.arange(w)
    inwin = (kc[None, :] >= c0[:, None]) & (kc[None, :] < c0[:, None] + NA_KW)
    dc = np.clip(kc[None, :] - cols[:, None] + (NA_KW - 1), 0, 2 * NA_KW - 2)
    off = np.arange(kh)
    a = np.arange(kh)
    dr = a[None, :] - off[:, None] + (NA_KH_MAX - 1)
    tab = rpb[:, dr[:, :, None, None], dc[None, None, :, :]]
    tab = jnp.where(jnp.asarray(inwin)[None, None, None], tab, MASKED_SCORE)
    tab = tab.transpose(1, 0, 3, 2, 4)
    return tab.reshape(kh, rpb.shape[0], w, kh * w).astype(F32)


def _attend(qm, keys, vals, biases):
    scores = [_dot_nt(qm, k) if bias is None else _dot_nt(qm, k) + bias for k, bias in zip(keys, biases)]
    m = functools.reduce(jnp.maximum, [jnp.max(s, axis=1, keepdims=True) for s in scores])
    es = [jnp.exp(s - m) for s in scores]
    den = functools.reduce(jnp.add, [jnp.sum(e, axis=1, keepdims=True) for e in es])
    o = functools.reduce(jnp.add, [_dot(e.astype(BF16), v) for e, v in zip(es, vals)])
    return o / den


def _heads_attention(q, key_groups, val_groups, bias_fn, o_ref):
    hpl = V7X_LANES // NA_HEAD_DIM
    lane = lax.broadcasted_iota(jnp.int32, (1, V7X_LANES), 1)
    for j in range(NA_HEADS // hpl):
        sl = slice(j * V7X_LANES, (j + 1) * V7X_LANES)
        q2 = q[:, sl]
        keys = [k[:, sl] for k in key_groups]
        vals = [v[:, sl] for v in val_groups]
        o2 = None
        for e in range(hpl):
            mine = (lane // NA_HEAD_DIM) == e
            qm = q2 * jnp.where(mine, 1.0, 0.0).astype(BF16)
            o = _attend(qm, keys, vals, bias_fn(j * hpl + e))
            o2 = o if o2 is None else jnp.where(mine, o, o2)
        o_ref[0, :, sl] = o2.astype(o_ref.dtype)


def _na_kernel(q_ref, k_ref, v_ref, kc_ref, vc_ref, bias_ref, o_ref, *, rows, kh):
    r = pl.program_id(1)
    rs = jnp.clip(r - kh // 2, 0, rows - kh)
    start = pl.multiple_of(rs * GRID_W, GRID_W)
    kwin = k_ref[0, pl.ds(start, kh * GRID_W), :]
    vwin = v_ref[0, pl.ds(start, kh * GRID_W), :]
    _heads_attention(q_ref[0], [kwin, kc_ref[0]], [vwin, vc_ref[0]],
                     lambda h: [bias_ref[0, h], None], o_ref)


def _neighbourhood_attention(qkv, qkv_c, rpb):
    b, t, w3 = qkv.shape
    c = qkv_c.shape[1]
    wd = w3 // 3
    rows = t // GRID_W
    kh = min(NA_KH_MAX, rows)
    bias = _na_bias_table(rpb, rows)

    def bias_map(i, r):
        return (r - jnp.clip(r - kh // 2, 0, rows - kh), 0, 0, 0)

    return pl.pallas_call(
        functools.partial(_na_kernel, rows=rows, kh=kh),
        out_shape=jax.ShapeDtypeStruct((b, t, wd), BF16),
        grid=(b, rows),
        in_specs=[pl.BlockSpec((1, GRID_W, wd), lambda i, r: (i, r, 0)),
                  pl.BlockSpec((1, t, wd), lambda i, r: (i, 0, 1)),
                  pl.BlockSpec((1, t, wd), lambda i, r: (i, 0, 2)),
                  pl.BlockSpec((1, c, wd), lambda i, r: (i, 0, 1)),
                  pl.BlockSpec((1, c, wd), lambda i, r: (i, 0, 2)),
                  pl.BlockSpec((1, NA_HEADS, GRID_W, kh * GRID_W), bias_map)],
        out_specs=pl.BlockSpec((1, GRID_W, wd), lambda i, r: (i, r, 0)),
        compiler_params=_cparams("parallel", "arbitrary"),
        name="na_attention",
    )(qkv, qkv, qkv, qkv_c, qkv_c, bias)


def _ctx_attn_kernel(q_ref, k_ref, v_ref, o_ref):
    _heads_attention(q_ref[0], [k_ref[0]], [v_ref[0]], lambda h: [None], o_ref)


def _context_attention(qkv_c):
    b, c, w3 = qkv_c.shape
    wd = w3 // 3
    return pl.pallas_call(
        _ctx_attn_kernel,
        out_shape=jax.ShapeDtypeStruct((b, c, wd), BF16),
        grid=(b,),
        in_specs=[pl.BlockSpec((1, c, wd), lambda i: (i, 0, 0)),
                  pl.BlockSpec((1, c, wd), lambda i: (i, 0, 1)),
                  pl.BlockSpec((1, c, wd), lambda i: (i, 0, 2))],
        out_specs=pl.BlockSpec((1, c, wd), lambda i: (i, 0, 0)),
        compiler_params=_cparams("parallel"),
        name="ctx_attention",
    )(qkv_c, qkv_c, qkv_c)


def _shift_rows(x, k):
    if k == 0:
        return x
    t = x.shape[0]
    rolled = pltpu.roll(x, k % t, axis=0)
    row = lax.broadcasted_iota(jnp.int32, x.shape, 0)
    valid = (row >= k) if k > 0 else (row < t + k)
    return jnp.where(valid, rolled, 0.0)


def _dwconv(x, w, b, pad_left):
    acc = None
    for k in range(w.shape[0]):
        term = _shift_rows(x, pad_left - k) * w[k:k + 1, :]
        acc = term if acc is None else acc + term
    return acc + b


def _hy_pre_kernel(x0_ref, x1_ref, v_ref, w0_ref, w1_ref, wv_ref, b0_ref, b1_ref, bv_ref, u_ref, x0c_ref):
    pad = HY_SHORT // 2
    x1c = _dwconv(x1_ref[0], w1_ref[...], b1_ref[...], pad)
    vc = _dwconv(v_ref[0], wv_ref[...], bv_ref[...], pad)
    u_ref[0] = vc * x1c
    x0c_ref[0] = _dwconv(x0_ref[0], w0_ref[...], b0_ref[...], pad)


def _hy_pre(rest, short_w, short_b):
    b, l, _ = rest.shape
    cw = V7X_LANES
    nh = HY_WIDTH // cw
    sb = short_b.reshape(1, -1)

    def xspec(part):
        return pl.BlockSpec((1, l, cw), lambda i, h: (i, 0, part * nh + h))

    def wspec(part, nrow):
        return pl.BlockSpec((nrow, cw), lambda i, h: (0, part * nh + h))

    return pl.pallas_call(
        _hy_pre_kernel,
        out_shape=(jax.ShapeDtypeStruct((b, l, HY_WIDTH), F32), jax.ShapeDtypeStruct((b, l, HY_WIDTH), F32)),
        grid=(b, nh),
        in_specs=[xspec(0), xspec(1), xspec(2),
                  wspec(0, HY_SHORT), wspec(1, HY_SHORT), wspec(2, HY_SHORT),
                  wspec(0, 1), wspec(1, 1), wspec(2, 1)],
        out_specs=(pl.BlockSpec((1, l, cw), lambda i, h: (i, 0, h)),
                   pl.BlockSpec((1, l, cw), lambda i, h: (i, 0, h))),
        compiler_params=_cparams("parallel", "parallel"),
        name="hyena_pre",
    )(rest, rest, rest, short_w, short_w, short_w, sb, sb, sb)


def _dft_factors(l):
    n2 = min(V7X_LANES, l // V7X_SUBLANES)
    n1 = 2 * l // n2
    assert n1 * n2 == 2 * l and n1 % 2 == 0 and n2 % V7X_SUBLANES == 0
    return n1, n2


def _dft_constants(n1, n2):
    n = n1 * n2
    f1 = np.arange(n1, dtype=np.float64)
    ang1 = 2.0 * np.pi * np.outer(f1, f1) / n1
    first = np.concatenate([np.cos(ang1), -np.sin(ang1)], axis=0)
    last = np.concatenate([np.cos(ang1), -np.sin(ang1)], axis=1) / n
    f2 = np.arange(n2, dtype=np.float64)
    ang2 = 2.0 * np.pi * np.outer(f2, f2) / n2
    angt = 2.0 * np.pi * np.outer(f1, f2) / n
    c = lambda a: jnp.asarray(a, F32)
    return dict(first=c(first), last=c(last), f2r=c(np.cos(ang2)), f2i=c(-np.sin(ang2)),
                twr=c(np.cos(angt)), twi=c(-np.sin(angt)))


def _dft_first_kernel(x_ref, f_ref, o_ref):
    o_ref[0] = _dot_f32(f_ref[...], x_ref[0])


def _dft_first(x, fmat):
    b, s1, m = x.shape
    tn = min(m, 4096)
    return pl.pallas_call(
        _dft_first_kernel,
        out_shape=jax.ShapeDtypeStruct((b, fmat.shape[0], m), F32),
        grid=(b, m // tn),
        in_specs=[pl.BlockSpec((1, s1, tn), lambda i, j: (i, 0, j)),
                  pl.BlockSpec(fmat.shape, lambda i, j: (0, 0))],
        out_specs=pl.BlockSpec((1, fmat.shape[0], tn), lambda i, j: (i, 0, j)),
        compiler_params=_cparams("parallel", "parallel"),
        name="dft_first",
    )(x, fmat)


def _dft_mid_kernel(*refs, with_filter, nb):
    if with_filter:
        a_ref, f2r_ref, f2i_ref, twr_ref, twi_ref, twrc_ref, twic_ref, k_ref, o_ref = refs
    else:
        a_ref, f2r_ref, f2i_ref, twr_ref, twi_ref, o_ref = refs
    n2 = f2r_ref.shape[0]
    f2r, f2i = f2r_ref[...], f2i_ref[...]
    twr, twi = twr_ref[0], twi_ref[0]
    mr = f2r * twr - f2i * twi
    mi = f2r * twi + f2i * twr
    mf = jnp.concatenate([jnp.concatenate([mr, -mi], axis=1), jnp.concatenate([mi, mr], axis=1)], axis=0)
    if with_filter:
        cr, ci = twrc_ref[0], twic_ref[0]
        nr = f2r * cr - f2i * ci
        ni = -(f2r * ci + f2i * cr)
        mb = jnp.concatenate([jnp.concatenate([nr, -ni], axis=1), jnp.concatenate([ni, nr], axis=1)], axis=0)
        kr, ki = k_ref[0, 0], k_ref[1, 0]
    for b in range(nb):
        st = jnp.concatenate([a_ref[b, 0, 0], a_ref[b, 1, 0]], axis=0)
        x = _dot_f32(mf, st)
        if with_filter:
            xr, xi = x[:n2], x[n2:]
            y = jnp.concatenate([xr * kr - xi * ki, xr * ki + xi * kr], axis=0)
            x = _dot_f32(mb, y)
        o_ref[b, 0, 0] = x[:n2]
        o_ref[b, 1, 0] = x[n2:]


def _dft_mid(a, consts, kspec=None):
    b, _, n1, n2, c = a.shape
    with_filter = kspec is not None
    nb = b
    blk = pl.BlockSpec((nb, 2, 1, n2, c), lambda f, i: (i, 0, f, 0, 0))
    sq = pl.BlockSpec((n2, n2), lambda f, i: (0, 0))
    rowv = pl.BlockSpec((1, 1, n2), lambda f, i: (f, 0, 0))
    colv = pl.BlockSpec((1, n2, 1), lambda f, i: (f, 0, 0))
    args = [a, consts["f2r"], consts["f2i"], consts["twr"].reshape(n1, 1, n2), consts["twi"].reshape(n1, 1, n2)]
    specs = [blk, sq, sq, rowv, rowv]
    if with_filter:
        args += [consts["twr"].reshape(n1, n2, 1), consts["twi"].reshape(n1, n2, 1), kspec]
        specs += [colv, colv, pl.BlockSpec((2, 1, n2, c), lambda f, i: (0, f, 0, 0))]
    return pl.pallas_call(
        functools.partial(_dft_mid_kernel, with_filter=with_filter, nb=nb),
        out_shape=jax.ShapeDtypeStruct(a.shape, F32),
        grid=(n1, b // nb),
        in_specs=specs,
        out_specs=blk,
        compiler_params=_cparams("parallel", "parallel"),
        name="dft_mid_conv" if with_filter else "dft_mid_fwd",
    )(*args)


def _dft_last_kernel(c_ref, g_ref, u_ref, x0_ref, d_ref, o_ref):
    y = _dot_f32(g_ref[...], c_ref[0])
    o_ref[0] = (x0_ref[0] * (y + u_ref[0] * d_ref[...])).astype(o_ref.dtype)


def _dft_last(cmid, gmat, u, x0c, d_tiled):
    b, r2, m = cmid.shape
    s1 = gmat.shape[0]
    tn = min(m, 4096)
    blk = pl.BlockSpec((1, s1, tn), lambda i, j: (i, 0, j))
    return pl.pallas_call(
        _dft_last_kernel,
        out_shape=jax.ShapeDtypeStruct((b, s1, m), BF16),
        grid=(b, m // tn),
        in_specs=[pl.BlockSpec((1, r2, tn), lambda i, j: (i, 0, j)),
                  pl.BlockSpec(gmat.shape, lambda i, j: (0, 0)),
                  blk, blk,
                  pl.BlockSpec((1, tn), lambda i, j: (0, j))],
        out_specs=blk,
        compiler_params=_cparams("parallel", "parallel"),
        name="dft_last",
    )(cmid, gmat, u, x0c, d_tiled)


def _hy_filter_kernel(z_ref, dec_ref, w1_ref, b1_ref, w2_ref, b2_ref, w3_ref, fr_ref, o_ref):
    fr = fr_ref[...]
    h = jnp.sin(fr[0:1] * (_dot_f32(z_ref[...], w1_ref[...]) + b1_ref[...]))
    h = jnp.sin(fr[1:2] * (_dot_f32(h, w2_ref[...]) + b2_ref[...]))
    o_ref[...] = _dot_f32(h, w3_ref[...]) * dec_ref[...]


def _hy_filter_features(l):
    t = np.linspace(0.0, 1.0, l)[:, None]
    w = (2.0 * np.pi / l) * np.arange(l)[:, None]
    bands = np.linspace(1e-4, HY_BANDS - 1, HY_BANDS)[None, :]
    z = np.concatenate([t, np.cos(bands * w), -np.sin(bands * w)], axis=-1)
    deltas = np.linspace(math.log(HY_TARGET) / HY_SLOW_DECAY, math.log(HY_TARGET) / HY_FAST_DECAY, HY_WIDTH)
    decay = np.exp(-t * np.abs(deltas)[None, :])
    rev = (l - np.arange(l)) % l
    z2 = np.concatenate([z, z[rev]], axis=0)
    dec_rev = decay[rev]
    dec_rev[0] = 0.0
    d2 = np.concatenate([decay, dec_rev], axis=0)
    kpad = -(-z2.shape[1] // V7X_SUBLANES) * V7X_SUBLANES
    z2 = np.pad(z2, ((0, 0), (0, kpad - z2.shape[1])))
    return jnp.asarray(z2, F32), jnp.asarray(d2, F32)


def _hy_filter(l, f1_w, f1_b, f2_w, f2_b, f3_w, freq):
    z2, d2 = _hy_filter_features(l)
    kpad = z2.shape[1]
    w1 = jnp.pad(f1_w, ((0, kpad - f1_w.shape[0]), (0, 0)))
    tl = min(l, 512)
    nt = l // tl
    hid = HY_HIDDEN
    full = lambda shape: pl.BlockSpec(shape, lambda h, i: (0, 0))
    return pl.pallas_call(
        _hy_filter_kernel,
        out_shape=jax.ShapeDtypeStruct((2 * l, HY_WIDTH), F32),
        grid=(2, nt),
        in_specs=[pl.BlockSpec((tl, kpad), lambda h, i: (h * nt + i, 0)),
                  pl.BlockSpec((tl, HY_WIDTH), lambda h, i: (h * nt + i, 0)),
                  full((kpad, hid)), full((1, hid)), full((hid, hid)), full((1, hid)),
                  pl.BlockSpec((hid, HY_WIDTH), lambda h, i: (0, h)),
                  full((2, hid))],
        out_specs=pl.BlockSpec((tl, HY_WIDTH), lambda h, i: (h * nt + i, 0)),
        compiler_params=_cparams("parallel", "parallel"),
        name="hyena_filter",
    )(z2, d2, w1, f1_b.reshape(1, hid), f2_w, f2_b.reshape(1, hid), f3_w, freq)


def _hyena(rest, short_w, short_b, f1_w, f1_b, f2_w, f2_b, f3_w, freq, d_skip):
    b, l, _ = rest.shape
    c = HY_WIDTH
    n1, n2 = _dft_factors(l)
    consts = _dft_constants(n1, n2)
    kfilt = _hy_filter(l, f1_w, f1_b, f2_w, f2_b, f3_w, freq)
    ka = _dft_first(kfilt.reshape(1, n1, n2 * c), consts["first"])
    kspec = _dft_mid(ka.reshape(1, 2, n1, n2, c), consts)[0]
    u, x0c = _hy_pre(rest, short_w, short_b)
    ua = _dft_first(u.reshape(b, n1 // 2, n2 * c), consts["first"][:, :n1 // 2])
    cm = _dft_mid(ua.reshape(b, 2, n1, n2, c), consts, kspec)
    d_tiled = jnp.tile(d_skip.reshape(1, c), (1, n2))
    yb = _dft_last(cm.reshape(b, 2 * n1, n2 * c), consts["last"][:n1 // 2],
                   u.reshape(b, n1 // 2, n2 * c), x0c.reshape(b, n1 // 2, n2 * c), d_tiled)
    return yb.reshape(b, l, c)


def _scan8(a, b, row, reverse):
    for s in (1, 2, 4):
        sh = (V7X_SUBLANES - s) if reverse else s
        a_sh = pltpu.roll(a, sh, axis=0)
        b_sh = pltpu.roll(b, sh, axis=0)
        valid = (row < V7X_SUBLANES - s) if reverse else (row >= s)
        b = jnp.where(valid, a * b_sh + b, b)
        a = jnp.where(valid, a * a_sh, a)
    return a, b


def _rglru_kernel(x_ref, gt_ref, cw_ref, cb_ref, wg_ref, bg_ref, lam_ref, h0f_ref, h0b_ref,
                  y_ref, hfl_ref, hbf_ref, xc_s, af_s, bf_s, ab_s, bb_s, hf_s, hb_s, *, chunk):
    l, cw = xc_s.shape
    xc_s[...] = _dwconv(x_ref[0], cw_ref[...], cb_ref[...], RG_CONV // 2)
    lam = lam_ref[...]
    sp = jnp.maximum(-lam, 0.0) + jnp.log1p(jnp.exp(-jnp.abs(lam)))

    def pre(i, carry):
        t0 = pl.multiple_of(i * chunk, chunk)
        xc = xc_s[pl.ds(t0, chunk), :]
        g = _dot(xc.astype(BF16), wg_ref[0]) + bg_ref[0]
        for d, (a_s, b_s) in enumerate(((af_s, bf_s), (ab_s, bb_s))):
            r = jax.nn.sigmoid(g[:, (2 * d) * cw:(2 * d + 1) * cw])
            ig = jax.nn.sigmoid(g[:, (2 * d + 1) * cw:(2 * d + 2) * cw])
            log_a = (-RG_C) * r * sp[d:d + 1]
            x2 = 2.0 * log_a
            one_minus_a2 = -(jnp.tanh(0.5 * x2) * (jnp.exp(x2) + 1.0))
            a_s[pl.ds(t0, chunk), :] = jnp.exp(log_a)
            b_s[pl.ds(t0, chunk), :] = jnp.sqrt(one_minus_a2) * (ig * xc)
        return carry

    lax.fori_loop(0, l // chunk, pre, 0)

    ng = l // V7X_SUBLANES
    row = lax.broadcasted_iota(jnp.int32, (V7X_SUBLANES, cw), 0)

    def step(g, carry):
        hf, hb = carry
        t0 = pl.multiple_of(g * V7X_SUBLANES, V7X_SUBLANES)
        a, b = _scan8(af_s[pl.ds(t0, V7X_SUBLANES), :], bf_s[pl.ds(t0, V7X_SUBLANES), :], row, False)
        h = a * hf + b
        hf_s[pl.ds(t0, V7X_SUBLANES), :] = h
        hf = h[V7X_SUBLANES - 1:V7X_SUBLANES, :]
        t1 = pl.multiple_of((ng - 1 - g) * V7X_SUBLANES, V7X_SUBLANES)
        a, b = _scan8(ab_s[pl.ds(t1, V7X_SUBLANES), :], bb_s[pl.ds(t1, V7X_SUBLANES), :], row, True)
        h = a * hb + b
        hb_s[pl.ds(t1, V7X_SUBLANES), :] = h
        hb = h[0:1, :]
        return hf, hb

    hf, hb = lax.fori_loop(0, ng, step, (h0f_ref[0], h0b_ref[0]))
    hfl_ref[0] = hf
    hbf_ref[0] = hb

    def post(i, carry):
        t0 = pl.multiple_of(i * chunk, chunk)
        hsum = hf_s[pl.ds(t0, chunk), :] + hb_s[pl.ds(t0, chunk), :]
        y_ref[0, pl.ds(t0, chunk), :] = (hsum * _gelu_tanh(gt_ref[0, pl.ds(t0, chunk), :])).astype(y_ref.dtype)
        return carry

    lax.fori_loop(0, l // chunk, post, 0)


def _rglru_gate_weights(wa, ba, wx, bx):
    cw = V7X_LANES
    nh = RG_WIDTH // cw
    bd = RG_WIDTH // RG_BLOCKS
    bph = cw // bd

    def half_mat(w, h):
        m = jnp.zeros((cw, cw), F32)
        for j in range(bph):
            m = m.at[j * bd:(j + 1) * bd, j * bd:(j + 1) * bd].set(w[h * bph + j])
        return m

    wg = jnp.stack([jnp.concatenate([half_mat(wa[0], h), half_mat(wx[0], h), half_mat(wa[1], h), half_mat(wx[1], h)],
                                    axis=1) for h in range(nh)])
    bg = jnp.stack([jnp.concatenate([v[h * cw:(h + 1) * cw] for v in (ba[0], bx[0], ba[1], bx[1])])[None]
                    for h in range(nh)])
    return wg.astype(BF16), bg


def _rglru(rest, xcol, gcol, conv_w, conv_b, wg, bg, lam, h0f, h0b):
    b, l, _ = rest.shape
    cw = V7X_LANES
    nh = RG_WIDTH // cw
    chunk = min(l, 512)
    vec = pl.BlockSpec((1, 1, cw), lambda i, h: (i, 0, h))
    scr = pltpu.VMEM((l, cw), F32)
    return pl.pallas_call(
        functools.partial(_rglru_kernel, chunk=chunk),
        out_shape=(jax.ShapeDtypeStruct((b, l, RG_WIDTH), BF16),
                   jax.ShapeDtypeStruct((b, 1, RG_WIDTH), F32),
                   jax.ShapeDtypeStruct((b, 1, RG_WIDTH), F32)),
        grid=(b, nh),
        in_specs=[pl.BlockSpec((1, l, cw), lambda i, h: (i, 0, xcol + h)),
                  pl.BlockSpec((1, l, cw), lambda i, h: (i, 0, gcol + h)),
                  pl.BlockSpec((RG_CONV, cw), lambda i, h: (0, h)),
                  pl.BlockSpec((1, cw), lambda i, h: (0, h)),
                  pl.BlockSpec((1, cw, 4 * cw), lambda i, h: (h, 0, 0)),
                  pl.BlockSpec((1, 1, 4 * cw), lambda i, h: (h, 0, 0)),
                  pl.BlockSpec((2, cw), lambda i, h: (0, h)),
                  vec, vec],
        out_specs=(pl.BlockSpec((1, l, cw), lambda i, h: (i, 0, h)), vec, vec),
        scratch_shapes=[scr] * 7,
        compiler_params=_cparams("parallel", "parallel"),
        name="rglru",
    )(rest, rest, conv_w, conv_b.reshape(1, -1), wg, bg, lam, h0f, h0b)


def _merge_kernel(x_ref, ya_ref, yb_ref, yc_ref, mod_ref, gm_ref, gf_ref, wg_ref, bg_ref,
                  wa_ref, wb_ref, wc_ref, wo_ref, xo_ref, xn_ref):
    d = x_ref.shape[-1]
    x = x_ref[0]
    mod = mod_ref[0]
    xn = _mod_norm(x, gm_ref[...], mod[1:2], mod[0:1])
    g = jax.nn.sigmoid(_dot(xn.astype(BF16), wg_ref[...]) + bg_ref[...])
    m = (g[:, :d] * _dot(ya_ref[0], wa_ref[...]) + g[:, d:2 * d] * _dot(yb_ref[0], wb_ref[...])
         + g[:, 2 * d:] * _dot(yc_ref[0], wc_ref[...]))
    x1 = x + mod[2:3] * _dot(m.astype(BF16), wo_ref[...])
    xo_ref[0] = x1
    xn_ref[0] = _mod_norm(x1, gf_ref[...], mod[4:5], mod[3:4]).astype(BF16)


def _merge(x, ya, yb, yc, mod, g_mix, g_ffn, w_gate, b_gate, w_a, w_b, w_c, w_out):
    b, t, d = x.shape
    tm = min(t, 512)
    tok = lambda w: pl.BlockSpec((1, tm, w), lambda i, j: (i, j, 0))
    full = lambda a: pl.BlockSpec(a.shape, lambda i, j: (0,) * a.ndim)
    gm, gf, bgate = g_mix.reshape(1, d), g_ffn.reshape(1, d), b_gate.reshape(1, -1)
    return pl.pallas_call(
        _merge_kernel,
        out_shape=(jax.ShapeDtypeStruct((b, t, d), F32), jax.ShapeDtypeStruct((b, t, d), BF16)),
        grid=(b, t // tm),
        in_specs=[tok(d), tok(ya.shape[-1]), tok(yb.shape[-1]), tok(yc.shape[-1]),
                  pl.BlockSpec((1, N_MOD, d), lambda i, j: (i, 0, 0)),
                  full(gm), full(gf), full(w_gate), full(bgate), full(w_a), full(w_b), full(w_c), full(w_out)],
        out_specs=(tok(d), tok(d)),
        compiler_params=_cparams("parallel", "parallel"),
        name="merge",
    )(x, ya, yb, yc, mod, gm, gf, w_gate, bgate, w_a, w_b, w_c, w_out)


def _peer_scores_kernel(xn_ref, wq_ref, keys_ref, s_ref):
    q = _dot(xn_ref[...], wq_ref[...])
    nk = keys_ref.shape[2]
    for g in range(keys_ref.shape[0]):
        s_ref[g] = _dot_nt(keys_ref[g], q[:, g * nk:(g + 1) * nk].astype(BF16))


def _peer_scores(xn, wq, keys):
    n, d = xn.shape
    ng, nk, dk = keys.shape
    tm = min(n, 512)
    return pl.pallas_call(
        _peer_scores_kernel,
        out_shape=jax.ShapeDtypeStruct((ng, nk, n), F32),
        grid=(n // tm,),
        in_specs=[pl.BlockSpec((tm, d), lambda i: (i, 0)),
                  pl.BlockSpec(wq.shape, lambda i: (0, 0)),
                  pl.BlockSpec(keys.shape, lambda i: (0, 0, 0))],
        out_specs=pl.BlockSpec((ng, nk, tm), lambda i: (0, 0, i)),
        compiler_params=_cparams("parallel"),
        name="peer_scores",
    )(xn, wq, keys)


def _top_sorted(w, k):
    out = []
    for it in range(k):
        m = jnp.max(w, axis=0, keepdims=True)
        out.append(m)
        if it + 1 < k:
            w = jnp.where(w == m, -jnp.inf, w)
    return out


def _peer_pairs():
    return [(a, b) for a in range(PEER_TOPK) for b in range(PEER_TOPK) if (a + 1) * (b + 1) <= PEER_TOPK]


def _peer_gate_kernel(s_ref, thr_ref, e1_ref, e2_ref, cand_s):
    pairs = _peer_pairs()
    cand_s[...] = jnp.full(cand_s.shape, -jnp.inf, F32)
    for h in range(PEER_HEADS):
        s1, s2 = s_ref[2 * h], s_ref[2 * h + 1]
        m1 = _top_sorted(s1, PEER_TOPK)
        m2 = _top_sorted(s2, PEER_TOPK)
        for idx, (a, b) in enumerate(pairs):
            cand_s[idx:idx + 1, :] = m1[a] + m2[b]
        cand = cand_s[...]
        thr = _top_sorted(cand, PEER_TOPK)[-1]
        cmax = m1[0] + m2[0]
        z = jnp.sum(jnp.where(cand >= thr, jnp.exp(cand - cmax), 0.0), axis=0, keepdims=True)
        thr_ref[h:h + 1, :] = thr
        e1_ref[h] = jnp.exp(s1 - m1[0]) / z
        e2_ref[h] = jnp.exp(s2 - m2[0])


def _peer_gates(s):
    ng, nk, n = s.shape
    tn = min(n, 256)
    npad = -(-len(_peer_pairs()) // V7X_SUBLANES) * V7X_SUBLANES
    return pl.pallas_call(
        _peer_gate_kernel,
        out_shape=(jax.ShapeDtypeStruct((PEER_HEADS, n), F32),
                   jax.ShapeDtypeStruct((PEER_HEADS, nk, n), F32),
                   jax.ShapeDtypeStruct((PEER_HEADS, nk, n), F32)),
        grid=(n // tn,),
        in_specs=[pl.BlockSpec((ng, nk, tn), lambda i: (0, 0, i))],
        out_specs=(pl.BlockSpec((PEER_HEADS, tn), lambda i: (0, i)),
                   pl.BlockSpec((PEER_HEADS, nk, tn), lambda i: (0, 0, i)),
                   pl.BlockSpec((PEER_HEADS, nk, tn), lambda i: (0, 0, i))),
        scratch_shapes=[pltpu.VMEM((npad, tn), F32)],
        compiler_params=_cparams("parallel"),
        name="peer_gates",
    )(s)


def _peer_dense_kernel(x_ref, xn_ref, mod_ref, s_ref, thr_ref, e1_ref, e2_ref, u_ref, vt_ref, o_ref,
                       acc_s, w_s, *, n_eb):
    nk = s_ref.shape[1]
    eb = pl.program_id(1)
    nblk = u_ref.shape[0] // nk

    @pl.when(eb == 0)
    def _():
        acc_s[...] = jnp.zeros(acc_s.shape, F32)

    xn = xn_ref[...]
    for ib in range(nblk):
        i = eb * nblk + ib
        ht = _dot_nt(u_ref[ib * nk:(ib + 1) * nk, :], xn)
        g = None
        for h in range(PEER_HEADS):
            s1row = s_ref[2 * h, pl.ds(i, 1), :]
            e1row = e1_ref[h, pl.ds(i, 1), :]
            picked = (s1row + s_ref[2 * h + 1]) >= thr_ref[h:h + 1, :]
            contrib = jnp.where(picked, e2_ref[h] * e1row, 0.0)
            g = contrib if g is None else g + contrib
        w_s[ib * nk:(ib + 1) * nk, :] = (_gelu_tanh(ht) * g).astype(BF16)
    acc_s[...] += _dot(vt_ref[...], w_s[...])

    @pl.when(eb == n_eb - 1)
    def _():
        o_ref[...] = x_ref[...] + mod_ref[0, N_MOD - 1:N_MOD, :] * acc_s[...].T


def _peer_dense(x, xn, mod, tokens_per_mod, s, thr, e1, e2, u_bf, vt_bf):
    n, d = x.shape
    ne = u_bf.shape[0]
    ng, nk, _ = s.shape
    tm = min(n, 512, tokens_per_mod)
    te = min(ne, 1024)
    n_eb = ne // te
    per = tokens_per_mod // tm
    return pl.pallas_call(
        functools.partial(_peer_dense_kernel, n_eb=n_eb),
        out_shape=jax.ShapeDtypeStruct((n, d), F32),
        grid=(n // tm, n_eb),
        in_specs=[pl.BlockSpec((tm, d), lambda i, e: (i, 0)),
                  pl.BlockSpec((tm, d), lambda i, e: (i, 0)),
                  pl.BlockSpec((1, N_MOD, d), lambda i, e: (i // per, 0, 0)),
                  pl.BlockSpec((ng, nk, tm), lambda i, e: (0, 0, i)),
                  pl.BlockSpec((PEER_HEADS, tm), lambda i, e: (0, i)),
                  pl.BlockSpec((PEER_HEADS, nk, tm), lambda i, e: (0, 0, i)),
                  pl.BlockSpec((PEER_HEADS, nk, tm), lambda i, e: (0, 0, i)),
                  pl.BlockSpec((te, d), lambda i, e: (e, 0)),
                  pl.BlockSpec((d, te), lambda i, e: (0, e))],
        out_specs=pl.BlockSpec((tm, d), lambda i, e: (i, 0)),
        scratch_shapes=[pltpu.VMEM((d, tm), F32), pltpu.VMEM((te, tm), BF16)],
        compiler_params=_cparams("parallel", "arbitrary"),
        name="peer_dense",
    )(x, xn, mod, s, thr, e1, e2, u_bf, vt_bf)


def _peer_layer(x, xn, mod, wq_bf, keys_bf, u_bf, vt_bf):
    b, t, d = x.shape
    xn2 = xn.reshape(b * t, d)
    s = _peer_scores(xn2, wq_bf, keys_bf)
    thr, e1, e2 = _peer_gates(s)
    out = _peer_dense(x.reshape(b * t, d), xn2, mod, t, s, thr, e1, e2, u_bf, vt_bf)
    return out.reshape(b, t, d)


def _final_norm_kernel(x_ref, g_ref, o_ref):
    x = x_ref[...]
    ms = jnp.mean(x * x, axis=-1, keepdims=True)
    o_ref[...] = x * lax.rsqrt(ms + EPS) * g_ref[...]


def _final_norm(x, g):
    b, t, d = x.shape
    n = b * t
    tm = min(n, 1024)
    out = pl.pallas_call(
        _final_norm_kernel,
        out_shape=jax.ShapeDtypeStruct((n, d), F32),
        grid=(n // tm,),
        in_specs=[pl.BlockSpec((tm, d), lambda i: (i, 0)), pl.BlockSpec((1, d), lambda i: (0, 0))],
        out_specs=pl.BlockSpec((tm, d), lambda i: (i, 0)),
        compiler_params=_cparams("parallel"),
        name="final_norm",
    )(x.reshape(n, d), g.reshape(1, d))
    return out.reshape(b, t, d)


def kernel(x, c, ctx, c_ctx, norm_mix_g, norm_ffn_g, w_ada, b_ada, w_in, na_rpb, hy_short_w, hy_short_b, hy_f1_w, hy_f1_b, hy_f2_w, hy_f2_b, hy_f3_w, hy_freq, hy_bias, rg_conv_w, rg_conv_b, rg_wa, rg_ba, rg_wx, rg_bx, rg_lambda, w_gate, b_gate, w_br_a, w_br_b, w_br_c, w_out, peer_wq, peer_keys, peer_u, peer_v, final_g):
    depth = w_ada.shape[0]
    b = x.shape[0]
    na_width = NA_HEADS * NA_HEAD_DIM
    hy_blocks = 3 * HY_WIDTH // V7X_LANES
    rg_blocks = RG_WIDTH // V7X_LANES
    mods, mods_c = _ada_mods(c, c_ctx, w_ada, b_ada)
    for l in range(depth):
        last = l == depth - 1
        mod, mod_c = mods[l], mods_c[l]
        w_in_bf = w_in[l].astype(BF16)
        hy_args = (hy_short_w[l], hy_short_b[l], hy_f1_w[l], hy_f1_b[l], hy_f2_w[l], hy_f2_b[l],
                   hy_f3_w[l], hy_freq[l], hy_bias[l])
        wg, bg = _rglru_gate_weights(rg_wa[l], rg_ba[l], rg_wx[l], rg_bx[l])
        merge_w = (norm_mix_g[l], norm_ffn_g[l], w_gate[l].astype(BF16), b_gate[l],
                   w_br_a[l].astype(BF16), w_br_b[l].astype(BF16), w_br_c[l].astype(BF16), w_out[l].astype(BF16))
        nk = PEER_N_KEYS
        peer_w = (peer_wq[l].astype(BF16),
                  peer_keys[l].reshape(2 * PEER_HEADS, nk, PEER_DK // 2).astype(BF16),
                  peer_u[l].astype(BF16), peer_v[l].astype(BF16).T)

        qkv, rest = _inproj(x, norm_mix_g[l], mod, w_in_bf)
        qkv_c, rest_c = _inproj(ctx, norm_mix_g[l], mod_c, w_in_bf)
        ya = _neighbourhood_attention(qkv, qkv_c, na_rpb[l])
        yb = _hyena(rest, *hy_args)
        h0 = jnp.zeros((b, 1, RG_WIDTH), F32)
        rg = lambda r, h0f, h0b: _rglru(r, hy_blocks, hy_blocks + rg_blocks, rg_conv_w[l], rg_conv_b[l],
                                        wg, bg, rg_lambda[l], h0f, h0b)
        yc_c, hf_last, hb_first = rg(rest_c, h0, h0)
        yc, _, _ = rg(rest, hf_last, hb_first)
        x, xn = _merge(x, ya, yb, yc, mod, *merge_w)
        if not last:
            ya_c = _context_attention(qkv_c)
            yb_c = _hyena(rest_c, *hy_args)
            ctx, cn = _merge(ctx, ya_c, yb_c, yc_c, mod_c, *merge_w)

        x = _peer_layer(x, xn, mod, *peer_w)
        if not last:
            ctx = _peer_layer(ctx, cn, mod_c, *peer_w)
    return _final_norm(x, final_g)
```
